```python
import jax, jax.numpy as jnp
from jax import lax
import numpy as np

D_MODEL = 2048
BATCH = 8
SEQ = 2048
DEPTH = 1
DEC_BATCH = 128
DEC_SEQ = 8
PAST_LEN = 16384
PAGE_SIZE = 128

D_CONV = D_MODEL // 2
CONV_W = 3
HEAD_DIM = 64
D_ATTN = D_MODEL // 2
N_HEADS = D_ATTN // HEAD_DIM
N_KV_HEADS = N_HEADS // 4
GROUP = N_HEADS // N_KV_HEADS
D_KV = N_KV_HEADS * HEAD_DIM
WINDOW = 128
BLOCK = WINDOW
LN_EPS = 1e-5
ALPHA = (2 * DEPTH) ** 0.25
BETA = (8 * DEPTH) ** -0.25
NEG = -1e30
SPLIT_SIZES = (D_CONV, D_CONV, D_CONV, D_CONV, D_ATTN, D_KV, D_KV, D_ATTN, D_MODEL, D_MODEL)
D_IN_TOTAL = sum(SPLIT_SIZES)
SPLIT_OFFSETS = tuple(int(o) for o in np.cumsum(SPLIT_SIZES)[:-1])

kernel_name = "hybrid_shortconv_swa_sink_alibi_deepnorm_step"


def layer_norm(x, g, b):
    xf = x.astype(jnp.float32)
    mu = jnp.mean(xf, axis=-1, keepdims=True)
    var = jnp.mean(jnp.square(xf - mu), axis=-1, keepdims=True)
    y = (xf - mu) * lax.rsqrt(var + LN_EPS) * g.astype(jnp.float32) + b.astype(jnp.float32)
    return y.astype(x.dtype)


def alibi_slopes():
    h = jnp.arange(1, N_HEADS + 1, dtype=jnp.float32)
    return (2.0 ** (-8.0 * h / N_HEADS)).reshape(N_KV_HEADS, GROUP)


def short_conv(u, prev, conv_w):
    L = u.shape[1]
    u_pad = jnp.concatenate([prev.astype(u.dtype), u], axis=1)
    y = conv_w[0] * u_pad[:, 0:L]
    for tap in range(1, CONV_W):
        y = y + conv_w[tap] * u_pad[:, tap:tap + L]
    return y, u_pad[:, -(CONV_W - 1):]


def sink_window_attend(q, k, v, delta, valid, sinks):
    s = jnp.einsum('...qkgd,...skd->...kgqs', q.astype(jnp.float32), k.astype(jnp.float32)) * (HEAD_DIM ** -0.5)
    s = s - alibi_slopes()[:, :, None, None] * delta
    s = jnp.where(valid, s, NEG)
    sink = sinks.astype(jnp.float32).reshape(N_KV_HEADS, GROUP)[:, :, None, None]
    m = jnp.maximum(jnp.max(s, axis=-1, keepdims=True), sink)
    p = jnp.exp(s - m)
    denom = jnp.sum(p, axis=-1, keepdims=True) + jnp.exp(sink - m)
    o = jnp.einsum('...kgqs,...skd->...qkgd', p / denom, v.astype(jnp.float32))
    return o.astype(q.dtype)


def prompt_window_attention(q, k, v, sinks):
    b, L = q.shape[0], q.shape[1]
    nb = L // BLOCK
    qb = q.reshape(b, nb, BLOCK, N_KV_HEADS, GROUP, HEAD_DIM)
    kb = k.reshape(b, nb, BLOCK, N_KV_HEADS, HEAD_DIM)
    vb = v.reshape(b, nb, BLOCK, N_KV_HEADS, HEAD_DIM)

    def band(t):
        prev = jnp.concatenate([jnp.zeros_like(t[:, :1]), t[:, :-1]], axis=1)
        return jnp.concatenate([prev, t], axis=2)

    i = jnp.arange(BLOCK)[:, None]
    j = jnp.arange(2 * BLOCK)[None, :]
    delta = BLOCK + i - j
    blk = jnp.arange(nb)[:, None, None]
    valid = (delta >= 0) & (delta <= WINDOW) & ((blk > 0) | (j >= BLOCK))
    o = sink_window_attend(qb, band(kb), band(vb), delta.astype(jnp.float32),
                           valid[:, None, None], sinks)
    return o.reshape(b, L, D_ATTN), k[:, -WINDOW:], v[:, -WINDOW:]


def sample_window_attention(q, k, v, sinks, cache_k, cache_v):
    b, lq = q.shape[0], q.shape[1]
    k_all = jnp.concatenate([cache_k.astype(k.dtype), k], axis=1)
    v_all = jnp.concatenate([cache_v.astype(v.dtype), v], axis=1)
    i = jnp.arange(lq)[:, None]
    j = jnp.arange(WINDOW + lq)[None, :]
    delta = WINDOW + i - j
    valid = (delta >= 0) & (delta <= WINDOW)
    o = sink_window_attend(q, k_all, v_all, delta.astype(jnp.float32), valid, sinks)
    return o.reshape(b, lq, D_ATTN), k_all[:, -WINDOW:], v_all[:, -WINDOW:]


def hybrid_layer(x, conv_prev, attend_fn, w_in, conv_w, w_conv_out, w_attn_out, w_out, ln_g, ln_b):
    bsz, L, _ = x.shape
    proj = jnp.einsum('bld,de->ble', x, w_in)
    gb, gc, h, z_c, q, k, v, z_a, gate_c, gate_a = jnp.split(proj, SPLIT_OFFSETS, axis=-1)
    conv_out, conv_state = short_conv(gc * h, conv_prev, conv_w)
    y_c = jax.nn.silu(z_c) * gb * conv_out
    q = q.reshape(bsz, L, N_KV_HEADS, GROUP, HEAD_DIM)
    k = k.reshape(bsz, L, N_KV_HEADS, HEAD_DIM)
    v = v.reshape(bsz, L, N_KV_HEADS, HEAD_DIM)
    o, k_state, v_state = attend_fn(q, k, v)
    y_a = jax.nn.silu(z_a) * o
    merged = (jax.nn.sigmoid(gate_c) * jnp.einsum('blc,cd->bld', y_c, w_conv_out)
              + jax.nn.sigmoid(gate_a) * jnp.einsum('bla,ad->bld', y_a, w_attn_out))
    out = jnp.einsum('bld,de->ble', merged, w_out)
    y = layer_norm(ALPHA * x + out, ln_g, ln_b)
    return y, k_state, v_state, conv_state


def setup_inputs(seed: int = 0) -> dict:
    key = jax.random.key(seed)
    ks = jax.random.split(key, 16)
    f32 = jnp.float32
    col_scale = jnp.asarray(np.concatenate([
        np.full(D_CONV, 1.0), np.full(D_CONV, 1.0), np.full(D_CONV, BETA), np.full(D_CONV, 1.0),
        np.full(D_ATTN, 1.0), np.full(D_KV, 1.0), np.full(D_KV, BETA), np.full(D_ATTN, 1.0),
        np.full(D_MODEL, 1.0), np.full(D_MODEL, 1.0)]).astype(np.float32))
    x_prompt = jax.random.normal(ks[0], (BATCH, SEQ, D_MODEL), f32)
    x_sample = jax.random.normal(ks[1], (DEC_BATCH, DEC_SEQ, D_MODEL), f32)
    cache_k = jax.random.normal(ks[2], (DEPTH, DEC_BATCH, WINDOW, N_KV_HEADS, HEAD_DIM), f32)
    cache_v = jax.random.normal(ks[3], (DEPTH, DEC_BATCH, WINDOW, N_KV_HEADS, HEAD_DIM), f32) * BETA
    state_conv = jax.random.normal(ks[4], (DEPTH, DEC_BATCH, CONV_W - 1, D_CONV), f32) * BETA
    w_in = jax.random.normal(ks[5], (DEPTH, D_MODEL, D_IN_TOTAL), f32) * (D_MODEL ** -0.5) * col_scale
    conv_w = jax.random.normal(ks[6], (DEPTH, CONV_W, D_CONV), f32) * (CONV_W ** -0.5)
    attn_sinks = jax.random.normal(ks[7], (DEPTH, N_HEADS), f32) * 0.5
    w_conv_out = jax.random.normal(ks[8], (DEPTH, D_CONV, D_MODEL), f32) * (D_CONV ** -0.5) * BETA
    w_attn_out = jax.random.normal(ks[9], (DEPTH, D_ATTN, D_MODEL), f32) * (D_ATTN ** -0.5) * BETA
    w_out = jax.random.normal(ks[10], (DEPTH, D_MODEL, D_MODEL), f32) * (D_MODEL ** -0.5) * BETA
    ln_g = 1.0 + 0.02 * jax.random.normal(ks[11], (DEPTH, D_MODEL), f32)
    ln_b = 0.02 * jax.random.normal(ks[12], (DEPTH, D_MODEL), f32)
    return {"x_prompt": x_prompt, "x_sample": x_sample, "cache_k": cache_k, "cache_v": cache_v,
            "state_conv": state_conv, "w_in": w_in, "conv_w": conv_w, "attn_sinks": attn_sinks,
            "w_conv_out": w_conv_out, "w_attn_out": w_attn_out, "w_out": w_out,
            "ln_g": ln_g, "ln_b": ln_b}


def reference(x_prompt, x_sample, cache_k, cache_v, state_conv, w_in, conv_w, attn_sinks,
              w_conv_out, w_attn_out, w_out, ln_g, ln_b):
    xp, xs = x_prompt, x_sample
    kp_l, vp_l, cp_l, ks_l, vs_l, cs_l = [], [], [], [], [], []
    for l in range(DEPTH):
        sinks = attn_sinks[l]
        conv_zero = jnp.zeros((xp.shape[0], CONV_W - 1, D_CONV), xp.dtype)
        xp, kp, vp, cp = hybrid_layer(
            xp, conv_zero,
            lambda q, k, v, s_=sinks: prompt_window_attention(q, k, v, s_),
            w_in[l], conv_w[l], w_conv_out[l], w_attn_out[l], w_out[l], ln_g[l], ln_b[l])
        ck, cv = cache_k[l], cache_v[l]
        xs, k_s, v_s, c_s = hybrid_layer(
            xs, state_conv[l],
            lambda q, k, v, s_=sinks, ck_=ck, cv_=cv: sample_window_attention(q, k, v, s_, ck_, cv_),
            w_in[l], conv_w[l], w_conv_out[l], w_attn_out[l], w_out[l], ln_g[l], ln_b[l])
        kp_l.append(kp); vp_l.append(vp); cp_l.append(cp)
        ks_l.append(k_s); vs_l.append(v_s); cs_l.append(c_s)
    k_prompt = jnp.stack(kp_l)
    v_prompt = jnp.stack(vp_l)
    conv_prompt = jnp.stack(cp_l)
    k_sample = jnp.stack(ks_l)
    v_sample = jnp.stack(vs_l)
    conv_sample = jnp.stack(cs_l)
    return (xp, xs, k_prompt, v_prompt, conv_prompt, k_sample, v_sample, conv_sample)
```

```python
import functools

import jax
import jax.numpy as jnp
from jax import lax
from jax.experimental import pallas as pl
from jax.experimental.pallas import tpu as pltpu

D_MODEL = 2048
D_CONV = D_MODEL // 2
D_ATTN = D_MODEL // 2
HEAD_DIM = 64
N_HEADS = D_ATTN // HEAD_DIM
N_KV = N_HEADS // 4
GROUP = N_HEADS // N_KV
D_KV = N_KV * HEAD_DIM
WINDOW = 128
CONV_W = 3
LN_EPS = 1e-5
NEG = -1e30
OFF_GB = 0
OFF_GC = OFF_GB + D_CONV
OFF_H = OFF_GC + D_CONV
OFF_ZC = OFF_H + D_CONV
OFF_Q = OFF_ZC + D_CONV
OFF_K = OFF_Q + D_ATTN
OFF_V = OFF_K + D_KV
OFF_ZA = OFF_V + D_KV
OFF_GATE_C = OFF_ZA + D_ATTN
OFF_GATE_A = OFF_GATE_C + D_MODEL
D_QKVZ = OFF_GATE_C - OFF_Q

V7X_VMEM_LIMIT_BYTES = 56 * 1024 * 1024
SUBLANES = 8

BF16 = jnp.bfloat16
F32 = jnp.float32


def _alibi_slope(head):
    return 2.0 ** (-8.0 * (head + 1) / N_HEADS)


def _dot(a, b):
    return jnp.dot(a, b, preferred_element_type=F32)


def _dot_nt(a, b):
    return lax.dot_general(a, b, (((1,), (1,)), ((), ())), preferred_element_type=F32)


def _sigmoid(x):
    return 1.0 / (1.0 + jnp.exp(-x))


def _silu(x):
    return x * _sigmoid(x)


def _params(semantics):
    return pltpu.CompilerParams(dimension_semantics=semantics, vmem_limit_bytes=V7X_VMEM_LIMIT_BYTES)


def _conv_taps(u, u1, u2, cw_ref, gb, z):
    cw = cw_ref[...]
    conv = cw[0:1] * u2 + cw[1:2] * u1 + cw[2:3] * u
    return _silu(z) * gb * conv


def _conv_prompt_kernel(x_ref, wgb_ref, wgc_ref, wh_ref, wz_ref, cw_ref, yc_ref, ulast_ref,
                        xb_scr, carry_scr):
    i = pl.program_id(1)
    j = pl.program_id(2)

    @pl.when(j == 0)
    def _():
        xb_scr[...] = x_ref[0].astype(BF16)

    xb = xb_scr[...]
    gb = _dot(xb, wgb_ref[...])
    u = _dot(xb, wgc_ref[...]) * _dot(xb, wh_ref[...])
    z = _dot(xb, wz_ref[...])
    rows, tc = u.shape

    @pl.when(i == 0)
    def _():
        carry_scr[j] = jnp.zeros((SUBLANES, tc), F32)

    prev8 = carry_scr[j]
    p0 = prev8[SUBLANES - 2:SUBLANES - 1]
    p1 = prev8[SUBLANES - 1:SUBLANES]
    row = lax.broadcasted_iota(jnp.int32, (rows, tc), 0)
    u1 = jnp.where(row == 0, p1, pltpu.roll(u, 1, 0))
    u2 = jnp.where(row == 0, p0, jnp.where(row == 1, p1, pltpu.roll(u, 2, 0)))
    yc_ref[0] = _conv_taps(u, u1, u2, cw_ref, gb, z).astype(BF16)
    last8 = u[rows - SUBLANES:rows]
    carry_scr[j] = last8
    ulast_ref[0] = last8


def _conv_sample_kernel(x_ref, wgb_ref, wgc_ref, wh_ref, wz_ref, cw_ref, e_ref, yc_ref, u_ref, xb_scr,
                        *, seq_len):
    assert seq_len & (seq_len - 1) == 0 and seq_len >= CONV_W - 1
    j = pl.program_id(0)

    @pl.when(j == 0)
    def _():
        xb_scr[...] = x_ref[...].astype(BF16)

    xb = xb_scr[...]
    gb = _dot(xb, wgb_ref[...])
    u = _dot(xb, wgc_ref[...]) * _dot(xb, wh_ref[...])
    z = _dot(xb, wz_ref[...])
    rows, tc = u.shape
    pos = lax.broadcasted_iota(jnp.int32, (rows, tc), 0) & (seq_len - 1)
    e = e_ref[...]
    e_next = pltpu.roll(e, rows - 1, 0)
    u1 = jnp.where(pos == 0, e_next, pltpu.roll(u, 1, 0))
    u2 = jnp.where(pos < 2, e, pltpu.roll(u, 2, 0))
    yc_ref[...] = _conv_taps(u, u1, u2, cw_ref, gb, z).astype(BF16)
    u_ref[...] = u


def _w_in_spec(off, tile, index_of):
    base = off // tile
    return pl.BlockSpec((D_MODEL, tile), lambda *g: (0, base + index_of(*g)))


def _conv_prompt(x, w_in, conv_w, *, rows, tc):
    nb, seq, _ = x.shape
    n_i, n_j = seq // rows, D_CONV // tc
    wspecs = [_w_in_spec(off, tc, lambda b, i, j: j) for off in (OFF_GB, OFF_GC, OFF_H, OFF_ZC)]
    return pl.pallas_call(
        _conv_prompt_kernel,
        grid=(nb, n_i, n_j),
        in_specs=[pl.BlockSpec((1, rows, D_MODEL), lambda b, i, j: (b, i, 0))] + wspecs
        + [pl.BlockSpec((CONV_W, tc), lambda b, i, j: (0, j))],
        out_specs=[pl.BlockSpec((1, rows, tc), lambda b, i, j: (b, i, j)),
                   pl.BlockSpec((1, SUBLANES, tc), lambda b, i, j: (b, i, j))],
        out_shape=[jax.ShapeDtypeStruct((nb, seq, D_CONV), BF16),
                   jax.ShapeDtypeStruct((nb, n_i * SUBLANES, D_CONV), F32)],
        scratch_shapes=[pltpu.VMEM((rows, D_MODEL), BF16), pltpu.VMEM((n_j, SUBLANES, tc), F32)],
        compiler_params=_params(("arbitrary", "arbitrary", "arbitrary")),
        name="conv_prompt",
    )(x, w_in, w_in, w_in, w_in, conv_w)


def _conv_sample(x2, w_in, conv_w, e, *, seq_len, tc):
    rows = x2.shape[0]
    n_j = D_CONV // tc
    wspecs = [_w_in_spec(off, tc, lambda j: j) for off in (OFF_GB, OFF_GC, OFF_H, OFF_ZC)]
    return pl.pallas_call(
        functools.partial(_conv_sample_kernel, seq_len=seq_len),
        grid=(n_j,),
        in_specs=[pl.BlockSpec((rows, D_MODEL), lambda j: (0, 0))] + wspecs
        + [pl.BlockSpec((CONV_W, tc), lambda j: (0, j)), pl.BlockSpec((rows, tc), lambda j: (0, j))],
        out_specs=[pl.BlockSpec((rows, tc), lambda j: (0, j)), pl.BlockSpec((rows, tc), lambda j: (0, j))],
        out_shape=[jax.ShapeDtypeStruct((rows, D_CONV), BF16), jax.ShapeDtypeStruct((rows, D_CONV), F32)],
        scratch_shapes=[pltpu.VMEM((rows, D_MODEL), BF16)],
        compiler_params=_params(("arbitrary",)),
        name="conv_sample",
    )(x2, w_in, w_in, w_in, w_in, conv_w, e)


def _softmax_with_sink(s, sink):
    m = jnp.maximum(jnp.max(s, axis=1, keepdims=True), sink)
    p = jnp.exp(s - m)
    denom = jnp.sum(p, axis=1, keepdims=True) + jnp.exp(sink - m)
    return p, denom


def _attn_prompt_kernel(sink_ref, x_ref, wq_ref, wk_ref, wv_ref, wz0_ref, wz1_ref, ya_ref, klast_ref,
                        vlast_ref, kb_scr, vb_scr, o_scr):
    i = pl.program_id(1)
    n_i = pl.num_programs(1)
    xb = x_ref[0].astype(BF16)
    rows = xb.shape[0]
    q = (_dot(xb, wq_ref[...]) * (HEAD_DIM ** -0.5)).astype(BF16)
    k = _dot(xb, wk_ref[...])
    v = _dot(xb, wv_ref[...])
    z0 = _dot(xb, wz0_ref[...])
    z1 = _dot(xb, wz1_ref[...])
    half = D_ATTN // 2

    @pl.when(i == 0)
    def _():
        kb_scr[0:WINDOW] = jnp.zeros((WINDOW, D_KV), BF16)
        vb_scr[0:WINDOW] = jnp.zeros((WINDOW, D_KV), BF16)

    kb_scr[WINDOW:WINDOW + rows] = k.astype(BF16)
    vb_scr[WINDOW:WINDOW + rows] = v.astype(BF16)

    qi_idx = lax.broadcasted_iota(jnp.int32, (WINDOW, 2 * WINDOW), 0)
    kj_idx = lax.broadcasted_iota(jnp.int32, (WINDOW, 2 * WINDOW), 1)
    dj = kj_idx - qi_idx
    neg_delta = (dj - WINDOW).astype(F32)
    band_valid = (dj >= 0) & (dj <= WINDOW)
    first_valid = band_valid & (kj_idx >= jnp.where(i > 0, 0, WINDOW))

    for qb in range(rows // WINDOW):
        valid = first_valid if qb == 0 else band_valid
        r0 = qb * WINDOW
        for kv in range(N_KV):
            c0 = kv * HEAD_DIM
            kband = kb_scr[r0:r0 + 2 * WINDOW, c0:c0 + HEAD_DIM]
            vband = vb_scr[r0:r0 + 2 * WINDOW, c0:c0 + HEAD_DIM]
            qs = jnp.concatenate(
                [q[r0:r0 + WINDOW, (kv * GROUP + g) * HEAD_DIM:(kv * GROUP + g + 1) * HEAD_DIM]
                 for g in range(GROUP)], axis=0)
            s_all = _dot_nt(qs, kband)
            ps, denoms = [], []
            for g in range(GROUP):
                head = kv * GROUP + g
                s = s_all[g * WINDOW:(g + 1) * WINDOW] + _alibi_slope(head) * neg_delta
                s = jnp.where(valid, s, NEG)
                p, denom = _softmax_with_sink(s, sink_ref[head])
                ps.append(p.astype(BF16))
                denoms.append(denom)
            o_all = _dot(jnp.concatenate(ps, axis=0), vband)
            for g in range(GROUP):
                head = kv * GROUP + g
                o_scr[r0:r0 + WINDOW, head * HEAD_DIM:(head + 1) * HEAD_DIM] = (
                    o_all[g * WINDOW:(g + 1) * WINDOW] / denoms[g])

    ya_ref[0, :, 0:half] = (_silu(z0) * o_scr[:, 0:half]).astype(BF16)
    ya_ref[0, :, half:D_ATTN] = (_silu(z1) * o_scr[:, half:D_ATTN]).astype(BF16)
    kb_scr[0:WINDOW] = kb_scr[rows:rows + WINDOW]
    vb_scr[0:WINDOW] = vb_scr[rows:rows + WINDOW]

    @pl.when(i == n_i - 1)
    def _():
        klast_ref[0] = k[rows - WINDOW:rows]
        vlast_ref[0] = v[rows - WINDOW:rows]


def _attn_prompt(x, w_in, sinks, *, rows):
    nb, seq, _ = x.shape
    n_i = seq // rows
    half = D_ATTN // 2
    return pl.pallas_call(
        _attn_prompt_kernel,
        grid=(nb, n_i),
        in_specs=[pl.BlockSpec(memory_space=pltpu.SMEM),
                  pl.BlockSpec((1, rows, D_MODEL), lambda b, i: (b, i, 0)),
                  pl.BlockSpec((D_MODEL, D_ATTN), lambda b, i: (0, OFF_Q // D_ATTN)),
                  pl.BlockSpec((D_MODEL, D_KV), lambda b, i: (0, OFF_K // D_KV)),
                  pl.BlockSpec((D_MODEL, D_KV), lambda b, i: (0, OFF_V // D_KV)),
                  pl.BlockSpec((D_MODEL, half), lambda b, i: (0, OFF_ZA // half)),
                  pl.BlockSpec((D_MODEL, half), lambda b, i: (0, OFF_ZA // half + 1))],
        out_specs=[pl.BlockSpec((1, rows, D_ATTN), lambda b, i: (b, i, 0)),
                   pl.BlockSpec((1, WINDOW, D_KV), lambda b, i: (b, 0, 0)),
                   pl.BlockSpec((1, WINDOW, D_KV), lambda b, i: (b, 0, 0))],
        out_shape=[jax.ShapeDtypeStruct((nb, seq, D_ATTN), BF16),
                   jax.ShapeDtypeStruct((nb, WINDOW, D_KV), F32),
                   jax.ShapeDtypeStruct((nb, WINDOW, D_KV), F32)],
        scratch_shapes=[pltpu.VMEM((WINDOW + rows, D_KV), BF16), pltpu.VMEM((WINDOW + rows, D_KV), BF16),
                        pltpu.VMEM((rows, D_ATTN), F32)],
        compiler_params=_params(("arbitrary", "arbitrary")),
        name="attn_prompt",
    )(sinks, x, w_in, w_in, w_in, w_in, w_in)


def _proj_kernel(x_ref, w_ref, o_ref):
    o_ref[...] = _dot(x_ref[...].astype(BF16), w_ref[...])


def _proj(x2, w_in, *, off, width, tn):
    rows = x2.shape[0]
    return pl.pallas_call(
        _proj_kernel,
        grid=(width // tn,),
        in_specs=[pl.BlockSpec((rows, D_MODEL), lambda j: (0, 0)), _w_in_spec(off, tn, lambda j: j)],
        out_specs=pl.BlockSpec((rows, tn), lambda j: (0, j)),
        out_shape=jax.ShapeDtypeStruct((rows, width), F32),
        compiler_params=_params(("arbitrary",)),
        name="proj_sample",
    )(x2, w_in)


def _attn_sample_kernel(sink_ref, qkvz_ref, ck_ref, cv_ref, ya_ref, ko_ref, vo_ref, kall_scr, vall_scr,
                        o_scr, *, seq_len):
    assert seq_len == SUBLANES, "one f32 sublane tile of new tokens per sequence"
    n_seq = ck_ref.shape[0]
    n_keys = kall_scr.shape[0]
    q_rows = GROUP * seq_len
    r_idx = lax.broadcasted_iota(jnp.int32, (q_rows, n_keys), 0)
    kj_idx = lax.broadcasted_iota(jnp.int32, (q_rows, n_keys), 1)
    delta = WINDOW + (r_idx & (seq_len - 1)) - kj_idx
    valid = (delta >= 0) & (delta <= WINDOW)
    neg_delta = (-delta).astype(F32)
    g_col = lax.shift_right_logical(lax.broadcasted_iota(jnp.int32, (q_rows, 1), 0),
                                    seq_len.bit_length() - 1)
    pad = n_keys - WINDOW - seq_len
    kall_scr[WINDOW + seq_len:n_keys] = jnp.zeros((pad, D_KV), F32)
    vall_scr[WINDOW + seq_len:n_keys] = jnp.zeros((pad, D_KV), F32)

    def one_seq(sq, carry):
        t0 = pl.multiple_of(sq * seq_len, seq_len)
        q = qkvz_ref[pl.ds(t0, seq_len), OFF_Q - OFF_Q:OFF_K - OFF_Q] * (HEAD_DIM ** -0.5)
        k_new = qkvz_ref[pl.ds(t0, seq_len), OFF_K - OFF_Q:OFF_V - OFF_Q]
        v_new = qkvz_ref[pl.ds(t0, seq_len), OFF_V - OFF_Q:OFF_ZA - OFF_Q]
        ck = ck_ref[sq]
        cv = cv_ref[sq]
        ko_ref[sq, 0:WINDOW - seq_len] = ck[seq_len:WINDOW]
        ko_ref[sq, WINDOW - seq_len:WINDOW] = k_new
        vo_ref[sq, 0:WINDOW - seq_len] = cv[seq_len:WINDOW]
        vo_ref[sq, WINDOW - seq_len:WINDOW] = v_new
        kall_scr[0:WINDOW] = ck
        kall_scr[WINDOW:WINDOW + seq_len] = k_new
        vall_scr[0:WINDOW] = cv
        vall_scr[WINDOW:WINDOW + seq_len] = v_new
        qb = q.astype(BF16)
        for kv in range(N_KV):
            c0 = kv * HEAD_DIM
            qs = jnp.concatenate(
                [qb[:, (kv * GROUP + g) * HEAD_DIM:(kv * GROUP + g + 1) * HEAD_DIM] for g in range(GROUP)],
                axis=0)
            s = _dot_nt(qs, kall_scr[:, c0:c0 + HEAD_DIM].astype(BF16))
            slope = jnp.zeros((q_rows, 1), F32)
            sink = jnp.zeros((q_rows, 1), F32)
            for g in range(GROUP):
                head = kv * GROUP + g
                slope = jnp.where(g_col == g, _alibi_slope(head), slope)
                sink = jnp.where(g_col == g, sink_ref[head], sink)
            s = jnp.where(valid, s + slope * neg_delta, NEG)
            p, denom = _softmax_with_sink(s, sink)
            o = _dot(p.astype(BF16), vall_scr[:, c0:c0 + HEAD_DIM].astype(BF16)) / denom
            for g in range(GROUP):
                head = kv * GROUP + g
                o_scr[pl.ds(t0, seq_len), head * HEAD_DIM:(head + 1) * HEAD_DIM] = (
                    o[g * seq_len:(g + 1) * seq_len])
        return carry

    lax.fori_loop(0, n_seq, one_seq, 0)
    z = qkvz_ref[:, OFF_ZA - OFF_Q:D_QKVZ]
    ya_ref[...] = (_silu(z) * o_scr[...]).astype(BF16)


def _attn_sample(qkvz, cache_k, cache_v, sinks, *, seq_len, n_seq):
    nb = cache_k.shape[0]
    rows = n_seq * seq_len
    n_keys = WINDOW + 2 * SUBLANES * ((seq_len + 2 * SUBLANES - 1) // (2 * SUBLANES))
    cache_spec = pl.BlockSpec((n_seq, WINDOW, D_KV), lambda s: (s, 0, 0))
    return pl.pallas_call(
        functools.partial(_attn_sample_kernel, seq_len=seq_len),
        grid=(nb // n_seq,),
        in_specs=[pl.BlockSpec(memory_space=pltpu.SMEM),
                  pl.BlockSpec((rows, D_QKVZ), lambda s: (s, 0)), cache_spec, cache_spec],
        out_specs=[pl.BlockSpec((rows, D_ATTN), lambda s: (s, 0)), cache_spec, cache_spec],
        out_shape=[jax.ShapeDtypeStruct((nb * seq_len, D_ATTN), BF16),
                   jax.ShapeDtypeStruct(cache_k.shape, F32), jax.ShapeDtypeStruct(cache_v.shape, F32)],
        scratch_shapes=[pltpu.VMEM((n_keys, D_KV), F32), pltpu.VMEM((n_keys, D_KV), F32),
                        pltpu.VMEM((rows, D_ATTN), F32)],
        compiler_params=_params(("arbitrary",)),
        name="attn_sample",
    )(sinks, qkvz, cache_k, cache_v)


def _merge_kernel(x_ref, yc_ref, ya_ref, wgc_ref, wga_ref, wco_ref, wao_ref, m_ref, xb_scr):
    j = pl.program_id(1)

    @pl.when(j == 0)
    def _():
        xb_scr[...] = x_ref[...].astype(BF16)

    xb = xb_scr[...]
    merged = (_sigmoid(_dot(xb, wgc_ref[...])) * _dot(yc_ref[...], wco_ref[...])
              + _sigmoid(_dot(xb, wga_ref[...])) * _dot(ya_ref[...], wao_ref[...]))
    m_ref[...] = merged.astype(BF16)


def _merge(x2, yc, ya, w_in, w_conv_out, w_attn_out, *, rows, tn):
    n_tok = x2.shape[0]
    tok = lambda i, j: (i, 0)
    col = lambda i, j: (0, j)
    return pl.pallas_call(
        _merge_kernel,
        grid=(n_tok // rows, D_MODEL // tn),
        in_specs=[pl.BlockSpec((rows, D_MODEL), tok), pl.BlockSpec((rows, D_CONV), tok),
                  pl.BlockSpec((rows, D_ATTN), tok),
                  _w_in_spec(OFF_GATE_C, tn, lambda i, j: j), _w_in_spec(OFF_GATE_A, tn, lambda i, j: j),
                  pl.BlockSpec((D_CONV, tn), col), pl.BlockSpec((D_ATTN, tn), col)],
        out_specs=pl.BlockSpec((rows, tn), lambda i, j: (i, j)),
        out_shape=jax.ShapeDtypeStruct((n_tok, D_MODEL), BF16),
        scratch_shapes=[pltpu.VMEM((rows, D_MODEL), BF16)],
        compiler_params=_params(("arbitrary", "arbitrary")),
        name="merge",
    )(x2, yc, ya, w_in, w_in, w_conv_out, w_attn_out)


def _out_kernel(m_ref, x_ref, wo_ref, g_ref, b_ref, y_ref, *, alpha):
    r = alpha * x_ref[...] + _dot(m_ref[...], wo_ref[...])
    mu = jnp.mean(r, axis=-1, keepdims=True)
    c = r - mu
    var = jnp.mean(c * c, axis=-1, keepdims=True)
    y_ref[...] = c * lax.rsqrt(var + LN_EPS) * g_ref[...] + b_ref[...]


def _out(merged, x2, w_out, ln_g, ln_b, *, rows, alpha):
    n_tok = x2.shape[0]
    tok = lambda i: (i, 0)
    const = lambda i: (0, 0)
    return pl.pallas_call(
        functools.partial(_out_kernel, alpha=alpha),
        grid=(n_tok // rows,),
        in_specs=[pl.BlockSpec((rows, D_MODEL), tok), pl.BlockSpec((rows, D_MODEL), tok),
                  pl.BlockSpec((D_MODEL, D_MODEL), const), pl.BlockSpec((1, D_MODEL), const),
                  pl.BlockSpec((1, D_MODEL), const)],
        out_specs=pl.BlockSpec((rows, D_MODEL), tok),
        out_shape=jax.ShapeDtypeStruct((n_tok, D_MODEL), F32),
        compiler_params=_params(("arbitrary",)),
        name="out_ln",
    )(merged, x2, w_out, ln_g, ln_b)


def kernel(x_prompt, x_sample, cache_k, cache_v, state_conv, w_in, conv_w, attn_sinks, w_conv_out,
           w_attn_out, w_out, ln_g, ln_b):
    depth = w_in.shape[0]
    alpha = (2 * depth) ** 0.25
    nb, seq, _ = x_prompt.shape
    nd, dseq, _ = x_sample.shape
    xp, xs = x_prompt, x_sample.reshape(nd * dseq, D_MODEL)
    kp_l, vp_l, cp_l, ks_l, vs_l, cs_l = [], [], [], [], [], []
    for l in range(depth):
        w_in_b = w_in[l].astype(BF16)
        w_co_b = w_conv_out[l].astype(BF16)
        w_ao_b = w_attn_out[l].astype(BF16)
        w_out_b = w_out[l].astype(BF16)
        g2, b2 = ln_g[l].reshape(1, D_MODEL), ln_b[l].reshape(1, D_MODEL)

        yc, ulast = _conv_prompt(xp, w_in_b, conv_w[l], rows=1024, tc=256)
        ya, klast, vlast = _attn_prompt(xp, w_in_b, attn_sinks[l], rows=512)
        xp2 = xp.reshape(nb * seq, D_MODEL)
        merged = _merge(xp2, yc.reshape(nb * seq, D_CONV), ya.reshape(nb * seq, D_ATTN), w_in_b, w_co_b,
                        w_ao_b, rows=1024, tn=256)
        xp = _out(merged, xp2, w_out_b, g2, b2, rows=512, alpha=alpha).reshape(nb, seq, D_MODEL)
        kp_l.append(klast.reshape(nb, WINDOW, N_KV, HEAD_DIM))
        vp_l.append(vlast.reshape(nb, WINDOW, N_KV, HEAD_DIM))
        cp_l.append(ulast[:, ulast.shape[1] - (CONV_W - 1):, :])

        e = jnp.pad(state_conv[l], ((0, 0), (0, dseq - (CONV_W - 1)), (0, 0))).reshape(nd * dseq, D_CONV)
        yc_s, u_s = _conv_sample(xs, w_in_b, conv_w[l], e, seq_len=dseq, tc=256)
        qkvz = _proj(xs, w_in_b, off=OFF_Q, width=D_QKVZ, tn=512)
        ya_s, k_s, v_s = _attn_sample(qkvz, cache_k[l].reshape(nd, WINDOW, D_KV),
                                      cache_v[l].reshape(nd, WINDOW, D_KV), attn_sinks[l],
                                      seq_len=dseq, n_seq=16)
        merged_s = _merge(xs, yc_s, ya_s, w_in_b, w_co_b, w_ao_b, rows=nd * dseq, tn=256)
        xs = _out(merged_s, xs, w_out_b, g2, b2, rows=512, alpha=alpha)
        ks_l.append(k_s.reshape(nd, WINDOW, N_KV, HEAD_DIM))
        vs_l.append(v_s.reshape(nd, WINDOW, N_KV, HEAD_DIM))
        cs_l.append(u_s.reshape(nd, dseq, D_CONV)[:, dseq - (CONV_W - 1):, :])

    return (xp, xs.reshape(nd, dseq, D_MODEL), jnp.stack(kp_l), jnp.stack(vp_l), jnp.stack(cp_l),
            jnp.stack(ks_l), jnp.stack(vs_l), jnp.stack(cs_l))
```
